```python
import jax, jax.numpy as jnp
from jax import lax
import numpy as np

D_MODEL = 1024
BATCH = 8
SEQ = 4096
DEPTH = 1

GRID_W = 64
CTX_LEN = 256
GMLP_WIDTH = 1024
GMLP_GROUPS = 8
GMLP_GROUP_CH = GMLP_WIDTH // GMLP_GROUPS
CHUNK = 128
NA_HEADS = 16
HEAD_DIM = 64
NA_WIDTH = NA_HEADS * HEAD_DIM
WIN_ROWS = 8
WIN_W = 16
ROPE_THETA = 10000.0
OFF_Q = 2 * GMLP_WIDTH
OFF_K = OFF_Q + NA_WIDTH
OFF_V = OFF_K + NA_WIDTH
OFF_GATE = OFF_V + NA_WIDTH
IN_WIDTH = OFF_GATE + 2 * D_MODEL
N_EXPERTS = 32
TOP_K = 4
EXPERT_FF = 1024
SWIGLU_LIMIT = 7.0
SWIGLU_ALPHA = 1.702
DISPATCH_BLOCK = 128
DEEPNORM_ALPHA = (2.0 * DEPTH) ** 0.25
DEEPNORM_BETA = (8.0 * DEPTH) ** -0.25
LN_EPS = 1e-5
NEG_INF = -1e30

kernel_name = "hybrid_gmlp_natten_moe_dit_block"


def layer_norm(x, g, b):
    xf = x.astype(jnp.float32)
    mu = jnp.mean(xf, axis=-1, keepdims=True)
    var = jnp.mean(jnp.square(xf - mu), axis=-1, keepdims=True)
    return ((xf - mu) * lax.rsqrt(var + LN_EPS) * g + b).astype(x.dtype)


def adaln_params(cond, w_ada, b_ada):
    m = jax.nn.silu(cond) @ w_ada + b_ada
    return jnp.split(m, 6, axis=-1)


def modulate(x, shift, scale):
    return x * (1.0 + scale) + shift


def split_heads(t):
    return t.reshape(t.shape[0], t.shape[1], NA_HEADS, HEAD_DIM)


def rope_axis(x, pos):
    n = x.shape[-1]
    half = n // 2
    inv = 1.0 / (ROPE_THETA ** (np.arange(half, dtype=np.float32) * 2.0 / n))
    ang = pos.astype(jnp.float32)[:, None] * inv[None, :]
    cos = jnp.cos(ang)[:, None, :].astype(x.dtype)
    sin = jnp.sin(ang)[:, None, :].astype(x.dtype)
    x1, x2 = x[..., :half], x[..., half:]
    return jnp.concatenate([x1 * cos - x2 * sin, x2 * cos + x1 * sin], axis=-1)


def rope_2d(x):
    t = jnp.arange(x.shape[1])
    xr = rope_axis(x[..., :HEAD_DIM // 2], t // GRID_W)
    xc = rope_axis(x[..., HEAD_DIM // 2:], t % GRID_W)
    return jnp.concatenate([xr, xc], axis=-1)


def spatial_gating(z_pre, ln_g, ln_b, ws, bs):
    b, length, _ = z_pre.shape
    z = jax.nn.gelu(z_pre, approximate=False)
    zu, zv = z[..., :GMLP_WIDTH], z[..., GMLP_WIDTH:]
    zv = layer_norm(zv, ln_g, ln_b)
    zv = zv.reshape(b, length // CHUNK, CHUNK, GMLP_GROUPS, GMLP_GROUP_CH)
    s = jnp.einsum('gpq,bnqgc->bnpgc', ws, zv) + bs.T[None, None, :, :, None]
    return zu * s.reshape(b, length, GMLP_WIDTH)


def na_column_tables():
    n_cb = GRID_W // WIN_W
    span = 2 * WIN_W
    cb = np.arange(n_cb)
    cs = np.clip(cb * WIN_W - WIN_W // 2, 0, GRID_W - span)
    col_idx = cs[:, None] + np.arange(span)[None, :]
    qc = cb[:, None] * WIN_W + np.arange(WIN_W)[None, :]
    c0 = np.clip(qc - WIN_W // 2, 0, GRID_W - WIN_W)
    kc = col_idx[:, None, :]
    mask = (kc >= c0[..., None]) & (kc < c0[..., None] + WIN_W)
    dc_idx = np.clip(kc - qc[..., None] + WIN_W - 1, 0, 2 * WIN_W - 2)
    return col_idx, mask, dc_idx


def neighbourhood_attention(q, k, v, k_ctx, v_ctx, rpb, rows):
    b, length, h, dh = q.shape
    kr = min(WIN_ROWS, rows)
    scale = HEAD_DIM ** -0.5
    col_idx, col_mask, dc_idx = na_column_tables()
    n_cb, span = col_idx.shape
    q_rot = rope_2d(q).reshape(b, rows, GRID_W, h, dh)
    k_grid = rope_2d(k).reshape(b, rows, GRID_W, h, dh)
    v_grid = v.reshape(b, rows, GRID_W, h, dh)
    q_rows = jnp.moveaxis(q_rot, 1, 0)
    qp_rows = jnp.moveaxis(q.reshape(b, rows, GRID_W, h, dh), 1, 0)
    mask_b = jnp.asarray(col_mask)[None, None, :, :, None, :]

    def row_block(args):
        q_r, qp_r, r = args
        r0 = jnp.clip(r - kr // 2, 0, rows - kr)
        kb = lax.dynamic_slice_in_dim(k_grid, r0, kr, axis=1)
        vb = lax.dynamic_slice_in_dim(v_grid, r0, kr, axis=1)
        kg = kb[:, :, col_idx]
        vg = vb[:, :, col_idx]
        qb = q_r.reshape(b, n_cb, WIN_W, h, dh)
        qpb = qp_r.reshape(b, n_cb, WIN_W, h, dh)
        s_lat = jnp.einsum('bjihd,brjthd->bhjirt', qb, kg).astype(jnp.float32) * scale
        dr_idx = r0 + jnp.arange(kr) - r + (WIN_ROWS - 1)
        bias = rpb[:, dr_idx][:, :, dc_idx]
        bias = jnp.transpose(bias, (0, 2, 3, 1, 4)).astype(jnp.float32)
        s_lat = jnp.where(mask_b, s_lat + bias[None], NEG_INF)
        s_ctx = jnp.einsum('bjihd,bnhd->bhjin', qpb, k_ctx).astype(jnp.float32) * scale
        s = jnp.concatenate([s_lat.reshape(b, h, n_cb, WIN_W, kr * span), s_ctx], axis=-1)
        p = jax.nn.softmax(s, axis=-1).astype(v.dtype)
        p_lat = p[..., :kr * span].reshape(b, h, n_cb, WIN_W, kr, span)
        p_ctx = p[..., kr * span:]
        o = (jnp.einsum('bhjirt,brjthd->bjihd', p_lat, vg)
             + jnp.einsum('bhjin,bnhd->bjihd', p_ctx, v_ctx))
        return o.reshape(b, GRID_W, h, dh)

    out = lax.map(row_block, (q_rows, qp_rows, jnp.arange(rows)))
    return jnp.moveaxis(out, 0, 1).reshape(b, length, h * dh)


def context_attention(q, k, v):
    s = jnp.einsum('bqhd,bkhd->bhqk', q, k).astype(jnp.float32) * (HEAD_DIM ** -0.5)
    p = jax.nn.softmax(s, axis=-1).astype(v.dtype)
    o = jnp.einsum('bhqk,bkhd->bqhd', p, v)
    return o.reshape(q.shape[0], q.shape[1], NA_WIDTH)


def merge_branches(y_a, y_b, gate_pre, w_pa, w_pb, w_out, b_out):
    g_a = jax.nn.sigmoid(gate_pre[..., :D_MODEL])
    g_b = jax.nn.sigmoid(gate_pre[..., D_MODEL:])
    return (g_a * (y_a @ w_pa) + g_b * (y_b @ w_pb)) @ w_out + b_out


def expert_block(xb, w1, b1, w2, b2):
    z = xb @ w1 + b1
    zg = jnp.minimum(z[..., :EXPERT_FF], SWIGLU_LIMIT)
    zl = jnp.clip(z[..., EXPERT_FF:], -SWIGLU_LIMIT, SWIGLU_LIMIT)
    act = zg * jax.nn.sigmoid(SWIGLU_ALPHA * zg) * (zl + 1.0)
    return act @ w2 + b2


def moe_ffn(h, router_w, router_b, w1, b1, w2, b2):
    n, d = h.shape
    logits = (h @ router_w + router_b).astype(jnp.float32)
    top_val, top_idx = lax.top_k(logits, TOP_K)
    gates = jax.nn.softmax(top_val, axis=-1).astype(h.dtype)
    n_assign = n * TOP_K
    flat_e = top_idx.reshape(-1)
    flat_tok = jnp.arange(n_assign) // TOP_K
    flat_g = gates.reshape(-1)
    order = jnp.argsort(flat_e)
    e_sorted = flat_e[order]
    counts = jnp.bincount(flat_e, length=N_EXPERTS)
    padded = (counts + DISPATCH_BLOCK - 1) // DISPATCH_BLOCK * DISPATCH_BLOCK
    start = jnp.cumsum(counts) - counts
    pend = jnp.cumsum(padded)
    pstart = pend - padded
    dest = pstart[e_sorted] + jnp.arange(n_assign) - start[e_sorted]
    n_blocks = (n_assign + N_EXPERTS * (DISPATCH_BLOCK - 1) + DISPATCH_BLOCK - 1) // DISPATCH_BLOCK
    n_rows = n_blocks * DISPATCH_BLOCK
    slot_tok = jnp.full((n_rows,), n, jnp.int32).at[dest].set(flat_tok[order])
    slot_gate = jnp.zeros((n_rows,), h.dtype).at[dest].set(flat_g[order])
    block_e = jnp.clip(jnp.searchsorted(pend, jnp.arange(n_blocks) * DISPATCH_BLOCK, side='right'), 0, N_EXPERTS - 1)
    h_pad = jnp.concatenate([h, jnp.zeros((1, d), h.dtype)], axis=0)
    xs = h_pad[slot_tok].reshape(n_blocks, DISPATCH_BLOCK, d)

    def run_block(args):
        xb, e = args
        return expert_block(xb, w1[e], b1[e], w2[e], b2[e])

    ys = lax.map(run_block, (xs, block_e)).reshape(n_rows, d)
    out = jnp.zeros((n + 1, d), h.dtype).at[slot_tok].add(ys * slot_gate[:, None])
    return out[:n]


def setup_inputs(seed: int = 0) -> dict:
    key = jax.random.key(seed)
    ks = jax.random.split(key, 31)

    def nrm(k, shape, s):
        return s * jax.random.normal(k, shape, jnp.float32)

    D = D_MODEL
    return {
        "x": nrm(ks[0], (BATCH, SEQ, D), 1.0),
        "c": nrm(ks[1], (BATCH, D), 1.0),
        "ctx": nrm(ks[2], (BATCH, CTX_LEN, D), 1.0),
        "c_ctx": nrm(ks[3], (D,), 1.0),
        "ln_x_g": 1.0 + nrm(ks[4], (D,), 0.01),
        "ln_x_b": nrm(ks[5], (D,), 0.01),
        "ln_ctx_g": 1.0 + nrm(ks[6], (D,), 0.01),
        "ln_ctx_b": nrm(ks[7], (D,), 0.01),
        "w_ada": nrm(ks[8], (DEPTH, D, 6 * D), 0.5 * D ** -0.5),
        "b_ada": nrm(ks[9], (DEPTH, 6 * D), 0.01),
        "w_in": nrm(ks[10], (DEPTH, D, IN_WIDTH), D ** -0.5),
        "b_in": nrm(ks[11], (DEPTH, IN_WIDTH), 0.01),
        "gmlp_ln_g": 1.0 + nrm(ks[12], (DEPTH, GMLP_WIDTH), 0.01),
        "gmlp_ln_b": nrm(ks[13], (DEPTH, GMLP_WIDTH), 0.01),
        "gmlp_ws": nrm(ks[14], (DEPTH, GMLP_GROUPS, CHUNK, CHUNK), CHUNK ** -0.5),
        "gmlp_bs": 1.0 + nrm(ks[15], (DEPTH, GMLP_GROUPS, CHUNK), 0.01),
        "na_rpb": nrm(ks[16], (DEPTH, NA_HEADS, 2 * WIN_ROWS - 1, 2 * WIN_W - 1), 0.1),
        "w_pa": nrm(ks[17], (DEPTH, GMLP_WIDTH, D), GMLP_WIDTH ** -0.5),
        "w_pb": nrm(ks[18], (DEPTH, NA_WIDTH, D), NA_WIDTH ** -0.5),
        "w_out": nrm(ks[19], (DEPTH, D, D), DEEPNORM_BETA * D ** -0.5),
        "b_out": nrm(ks[20], (DEPTH, D), 0.01),
        "ln1_g": 1.0 + nrm(ks[21], (DEPTH, D), 0.01),
        "ln1_b": nrm(ks[22], (DEPTH, D), 0.01),
        "ln2_g": 1.0 + nrm(ks[23], (DEPTH, D), 0.01),
        "ln2_b": nrm(ks[24], (DEPTH, D), 0.01),
        "router_w": nrm(ks[25], (DEPTH, D, N_EXPERTS), D ** -0.5),
        "router_b": nrm(ks[26], (DEPTH, N_EXPERTS), 0.01),
        "exp_w1": nrm(ks[27], (DEPTH, N_EXPERTS, D, 2 * EXPERT_FF), D ** -0.5),
        "exp_b1": nrm(ks[28], (DEPTH, N_EXPERTS, 2 * EXPERT_FF), 0.01),
        "exp_w2": nrm(ks[29], (DEPTH, N_EXPERTS, EXPERT_FF, D), DEEPNORM_BETA * EXPERT_FF ** -0.5),
        "exp_b2": nrm(ks[30], (DEPTH, N_EXPERTS, D), 0.01),
    }


def reference(x, c, ctx, c_ctx, ln_x_g, ln_x_b, ln_ctx_g, ln_ctx_b, w_ada, b_ada, w_in, b_in,
              gmlp_ln_g, gmlp_ln_b, gmlp_ws, gmlp_bs, na_rpb, w_pa, w_pb, w_out, b_out,
              ln1_g, ln1_b, ln2_g, ln2_b, router_w, router_b, exp_w1, exp_b1, exp_w2, exp_b2):
    batch, length, d = x.shape
    rows = length // GRID_W
    x = layer_norm(x, ln_x_g, ln_x_b)
    ctx = layer_norm(ctx, ln_ctx_g, ln_ctx_b)
    for l in range(DEPTH):
        last = l == DEPTH - 1
        sh1, sc1, g1, sh2, sc2, g2 = adaln_params(c, w_ada[l], b_ada[l])
        sh1c, sc1c, g1c, sh2c, sc2c, g2c = adaln_params(c_ctx, w_ada[l], b_ada[l])
        u = modulate(x, sh1[:, None], sc1[:, None])
        u_c = modulate(ctx, sh1c, sc1c)
        proj = u @ w_in[l] + b_in[l]
        if last:
            kv_c = u_c @ w_in[l][:, OFF_K:OFF_GATE] + b_in[l][OFF_K:OFF_GATE]
        else:
            proj_c = u_c @ w_in[l] + b_in[l]
            kv_c = proj_c[..., OFF_K:OFF_GATE]
        k_c = split_heads(kv_c[..., :NA_WIDTH])
        v_c = split_heads(kv_c[..., NA_WIDTH:])
        y_a = spatial_gating(proj[..., :OFF_Q], gmlp_ln_g[l], gmlp_ln_b[l], gmlp_ws[l], gmlp_bs[l])
        y_b = neighbourhood_attention(split_heads(proj[..., OFF_Q:OFF_K]),
                                      split_heads(proj[..., OFF_K:OFF_V]),
                                      split_heads(proj[..., OFF_V:OFF_GATE]),
                                      k_c, v_c, na_rpb[l], rows)
        mix = merge_branches(y_a, y_b, proj[..., OFF_GATE:], w_pa[l], w_pb[l], w_out[l], b_out[l])
        x_new = layer_norm(DEEPNORM_ALPHA * x + g1[:, None] * mix, ln1_g[l], ln1_b[l])
        u2 = modulate(x_new, sh2[:, None], sc2[:, None])
        ff = moe_ffn(u2.reshape(batch * length, d), router_w[l], router_b[l],
                     exp_w1[l], exp_b1[l], exp_w2[l], exp_b2[l]).reshape(batch, length, d)
        x_new = layer_norm(DEEPNORM_ALPHA * x_new + g2[:, None] * ff, ln2_g[l], ln2_b[l])
        if not last:
            y_a_c = spatial_gating(proj_c[..., :OFF_Q], gmlp_ln_g[l], gmlp_ln_b[l], gmlp_ws[l], gmlp_bs[l])
            y_b_c = context_attention(split_heads(proj_c[..., OFF_Q:OFF_K]), k_c, v_c)
            mix_c = merge_branches(y_a_c, y_b_c, proj_c[..., OFF_GATE:], w_pa[l], w_pb[l], w_out[l], b_out[l])
            ctx = layer_norm(DEEPNORM_ALPHA * ctx + g1c * mix_c, ln1_g[l], ln1_b[l])
            u2c = modulate(ctx, sh2c, sc2c)
            ff_c = moe_ffn(u2c.reshape(-1, d), router_w[l], router_b[l],
                           exp_w1[l], exp_b1[l], exp_w2[l], exp_b2[l]).reshape(ctx.shape)
            ctx = layer_norm(DEEPNORM_ALPHA * ctx + g2c * ff_c, ln2_g[l], ln2_b[l])
        x = x_new
    return x
```

```python
import functools

import numpy as np
import jax
import jax.numpy as jnp
from jax import lax
from jax.experimental import pallas as pl
from jax.experimental.pallas import tpu as pltpu

F32 = jnp.float32
BF16 = jnp.bfloat16

D_MODEL = 1024
GRID_W = 64
GMLP_WIDTH = 1024
GMLP_GROUPS = 8
GMLP_GROUP_CH = GMLP_WIDTH // GMLP_GROUPS
CHUNK = 128
NA_HEADS = 16
HEAD_DIM = 64
NA_WIDTH = NA_HEADS * HEAD_DIM
WIN_ROWS = 8
WIN_W = 16
ROPE_THETA = 10000.0
OFF_Q = 2 * GMLP_WIDTH
OFF_K = OFF_Q + NA_WIDTH
OFF_V = OFF_K + NA_WIDTH
OFF_GATE = OFF_V + NA_WIDTH
IN_WIDTH = OFF_GATE + 2 * D_MODEL
N_EXPERTS = 32
TOP_K = 4
EXPERT_FF = 1024
SWIGLU_LIMIT = 7.0
SWIGLU_ALPHA = 1.702
LN_EPS = 1e-5
NEG_INF = -1e30

LANES = 128
HEADS_PER_STEP = LANES // HEAD_DIM
Q_ROWS = 4
K_ROWS = Q_ROWS + WIN_ROWS
IN_TM = 256
MERGE_TM = 512
EXPERT_TM = 512
VMEM_LIMIT = 56 * 1024 * 1024


def _layer_norm(x, g, b):
    mu = jnp.mean(x, axis=-1, keepdims=True)
    xc = x - mu
    var = jnp.mean(xc * xc, axis=-1, keepdims=True)
    return xc * lax.rsqrt(var + LN_EPS) * g + b


def _sigmoid(x):
    return 1.0 / (1.0 + jnp.exp(-x))


def _dot(a, b):
    return jnp.dot(a, b, preferred_element_type=F32)


def _dot_nt(a, b):
    return lax.dot_general(a, b, (((1,), (1,)), ((), ())), preferred_element_type=F32)


def _adaln_kernel(c_ref, w_ref, b_ref, o_ref):
    c = c_ref[...]
    s = c * _sigmoid(c)
    w = w_ref[...]
    s_hi = s.astype(BF16)
    s_lo = (s - s_hi.astype(F32)).astype(BF16)
    w_hi = w.astype(BF16)
    w_lo = (w - w_hi.astype(F32)).astype(BF16)
    acc = _dot(s_hi, w_hi) + _dot(s_hi, w_lo) + _dot(s_lo, w_hi)
    o_ref[...] = acc + b_ref[...]


def _adaln(cond, w_ada, b_ada):
    rows, d = cond.shape
    n = w_ada.shape[1]
    tn = 1536
    return pl.pallas_call(
        _adaln_kernel,
        grid=(n // tn,),
        in_specs=[
            pl.BlockSpec((rows, d), lambda j: (0, 0)),
            pl.BlockSpec((d, tn), lambda j: (0, j)),
            pl.BlockSpec((1, tn), lambda j: (0, j)),
        ],
        out_specs=pl.BlockSpec((rows, tn), lambda j: (0, j)),
        out_shape=jax.ShapeDtypeStruct((rows, n), F32),
        compiler_params=pltpu.CompilerParams(
            dimension_semantics=("arbitrary",), vmem_limit_bytes=VMEM_LIMIT),
        name="adaln",
    )(cond, w_ada, b_ada.reshape(1, n))


def _rope(t, cos, sin_signed):
    quarter = HEAD_DIM // 4
    lane = lax.broadcasted_iota(jnp.int32, (1, LANES), 1)
    first = (lane % (2 * quarter)) < quarter
    outs = []
    for g in range(t.shape[1] // LANES):
        tg = t[:, g * LANES:(g + 1) * LANES]
        fwd = pltpu.roll(tg, LANES - quarter, 1)
        bwd = pltpu.roll(tg, quarter, 1)
        rot = jnp.where(first, fwd, bwd)
        outs.append(tg * cos + rot * sin_signed)
    return outs


def _in_proj_kernel(x_ref, lng_ref, lnb_ref, sh_ref, sc_ref, w_ref, b_ref,
                    glg_ref, glb_ref, ws_ref, bs_ref, cos_ref, sin_ref,
                    ya_ref, qr_ref, qp_ref, kr_ref, v_ref, ga_ref, gb_ref):
    tm = x_ref.shape[1]
    xn = _layer_norm(x_ref[0], lng_ref[...], lnb_ref[...])
    u = (xn * (1.0 + sc_ref[0]) + sh_ref[0]).astype(BF16)

    def seg(off, width=GMLP_WIDTH):
        return _dot(u, w_ref[:, off:off + width]) + b_ref[:, off:off + width]

    def gelu(z):
        return 0.5 * z * (1.0 + lax.erf(z * (2.0 ** -0.5)))

    zu = gelu(seg(0))
    zv = _layer_norm(gelu(seg(GMLP_WIDTH)), glg_ref[...], glb_ref[...]).astype(BF16)
    for ci in range(tm // CHUNK):
        r0 = ci * CHUNK
        for g in range(GMLP_GROUPS):
            c0 = g * GMLP_GROUP_CH
            s = _dot(ws_ref[g], zv[r0:r0 + CHUNK, c0:c0 + GMLP_GROUP_CH]) + bs_ref[g]
            ya_ref[0, r0:r0 + CHUNK, c0:c0 + GMLP_GROUP_CH] = (
                zu[r0:r0 + CHUNK, c0:c0 + GMLP_GROUP_CH] * s).astype(BF16)

    cos = cos_ref[...]
    sin = sin_ref[...]
    scale = HEAD_DIM ** -0.5
    q = seg(OFF_Q) * scale
    qp_ref[0] = q.astype(BF16)
    for g, t in enumerate(_rope(q, cos, sin)):
        qr_ref[0, :, g * LANES:(g + 1) * LANES] = t.astype(BF16)
    for g, t in enumerate(_rope(seg(OFF_K), cos, sin)):
        kr_ref[0, :, g * LANES:(g + 1) * LANES] = t.astype(BF16)
    v_ref[0] = seg(OFF_V).astype(BF16)

    ga_ref[0] = _sigmoid(seg(OFF_GATE)).astype(BF16)
    gb_ref[0] = _sigmoid(seg(OFF_GATE + D_MODEL)).astype(BF16)


def _rope_tables(length):
    quarter = HEAD_DIM // 4
    half = HEAD_DIM // 2
    inv = 1.0 / (ROPE_THETA ** (np.arange(quarter, dtype=np.float32) * 2.0 / half))
    t = np.arange(length)
    ang_r = (t // GRID_W).astype(np.float32)[:, None] * inv[None, :]
    ang_c = (t % GRID_W).astype(np.float32)[:, None] * inv[None, :]
    cos = np.concatenate([np.cos(ang_r), np.cos(ang_r), np.cos(ang_c), np.cos(ang_c)], axis=1)
    sin = np.concatenate([-np.sin(ang_r), np.sin(ang_r), -np.sin(ang_c), np.sin(ang_c)], axis=1)
    reps = LANES // HEAD_DIM
    return (jnp.asarray(np.tile(cos, (1, reps)), F32), jnp.asarray(np.tile(sin, (1, reps)), F32))


def _in_proj(x, ln_g, ln_b, sh1, sc1, w_bf, b_in, glg, glb, ws_bf, bs_full, cos, sin):
    batch, length, d = x.shape
    tm = min(IN_TM, length)
    row = lambda a: a.reshape(1, -1)
    tile = pl.BlockSpec((1, tm, d), lambda i, b: (b, i, 0))
    vec = pl.BlockSpec((1, d), lambda i, b: (0, 0))
    per_b = pl.BlockSpec((1, 1, d), lambda i, b: (b, 0, 0))
    out_sd = jax.ShapeDtypeStruct((batch, length, d), BF16)
    return pl.pallas_call(
        _in_proj_kernel,
        grid=(length // tm, batch),
        in_specs=[
            tile, vec, vec, per_b, per_b,
            pl.BlockSpec((d, IN_WIDTH), lambda i, b: (0, 0)),
            pl.BlockSpec((1, IN_WIDTH), lambda i, b: (0, 0)),
            vec, vec,
            pl.BlockSpec((GMLP_GROUPS, CHUNK, CHUNK), lambda i, b: (0, 0, 0)),
            pl.BlockSpec((GMLP_GROUPS, CHUNK, GMLP_GROUP_CH), lambda i, b: (0, 0, 0)),
            pl.BlockSpec((tm, LANES), lambda i, b: (i, 0)),
            pl.BlockSpec((tm, LANES), lambda i, b: (i, 0)),
        ],
        out_specs=[tile] * 7,
        out_shape=[out_sd] * 7,
        compiler_params=pltpu.CompilerParams(
            dimension_semantics=("arbitrary", "arbitrary"), vmem_limit_bytes=VMEM_LIMIT),
        name="in_proj",
    )(x, row(ln_g), row(ln_b), sh1, sc1, w_bf, row(b_in), row(glg), row(glb),
      ws_bf, bs_full, cos, sin)


def _ctx_kv_kernel(c_ref, lng_ref, lnb_ref, sh_ref, sc_ref, wk_ref, wv_ref, bk_ref, bv_ref,
                   k_ref, v_ref):
    cn = _layer_norm(c_ref[0], lng_ref[...], lnb_ref[...])
    u = (cn * (1.0 + sc_ref[...]) + sh_ref[...]).astype(BF16)
    k_ref[0] = (_dot(u, wk_ref[...]) + bk_ref[...]).astype(BF16)
    v_ref[0] = (_dot(u, wv_ref[...]) + bv_ref[...]).astype(BF16)


def _ctx_kv(ctx, ln_g, ln_b, sh, sc, w_bf, b_in):
    batch, lc, d = ctx.shape
    row = lambda a: a.reshape(1, -1)
    vec = pl.BlockSpec((1, d), lambda b: (0, 0))
    tile = pl.BlockSpec((1, lc, d), lambda b: (b, 0, 0))
    kblk, vblk = OFF_K // NA_WIDTH, OFF_V // NA_WIDTH
    b2 = row(b_in)
    return pl.pallas_call(
        _ctx_kv_kernel,
        grid=(batch,),
        in_specs=[
            tile, vec, vec, vec, vec,
            pl.BlockSpec((d, NA_WIDTH), lambda b: (0, kblk)),
            pl.BlockSpec((d, NA_WIDTH), lambda b: (0, vblk)),
            pl.BlockSpec((1, NA_WIDTH), lambda b: (0, kblk)),
            pl.BlockSpec((1, NA_WIDTH), lambda b: (0, vblk)),
        ],
        out_specs=[tile, tile],
        out_shape=[jax.ShapeDtypeStruct((batch, lc, NA_WIDTH), BF16)] * 2,
        compiler_params=pltpu.CompilerParams(
            dimension_semantics=("arbitrary",), vmem_limit_bytes=VMEM_LIMIT),
        name="ctx_kv",
    )(ctx, row(ln_g), row(ln_b), row(sh), row(sc), w_bf, w_bf, b2, b2)


def _attn_bias_tables(rpb, rows):
    n_tiles = rows // Q_ROWS

    def pattern(t):
        k0 = int(np.clip(Q_ROWS * t - WIN_ROWS // 2, 0, rows - K_ROWS))
        r = Q_ROWS * t + np.arange(Q_ROWS)
        r0 = np.clip(r - WIN_ROWS // 2, 0, rows - WIN_ROWS)
        kr = k0 + np.arange(K_ROWS)
        c = np.arange(GRID_W)
        c0 = np.clip(c - WIN_W // 2, 0, GRID_W - WIN_W)
        kc = np.arange(GRID_W)
        ok_r = (kr[None, :] >= r0[:, None]) & (kr[None, :] < r0[:, None] + WIN_ROWS)
        ok_c = (kc[None, :] >= c0[:, None]) & (kc[None, :] < c0[:, None] + WIN_W)
        dr = np.clip(kr[None, :] - r[:, None] + WIN_ROWS - 1, 0, 2 * WIN_ROWS - 2)
        dc = np.clip(kc[None, :] - c[:, None] + WIN_W - 1, 0, 2 * WIN_W - 2)
        shape = (Q_ROWS, GRID_W, K_ROWS, GRID_W)
        ok = np.broadcast_to(ok_r[:, None, :, None] & ok_c[None, :, None, :], shape)
        dr_i = np.broadcast_to(dr[:, None, :, None], shape)
        dc_i = np.broadcast_to(dc[None, :, None, :], shape)
        n_q, n_k = Q_ROWS * GRID_W, K_ROWS * GRID_W
        return ok.reshape(n_q, n_k), dr_i.reshape(n_q, n_k), dc_i.reshape(n_q, n_k)

    pats = [pattern(min(1, n_tiles - 1)), pattern(0), pattern(n_tiles - 1)]
    ok = np.stack([p[0] for p in pats])
    dr = np.stack([p[1] for p in pats])
    dc = np.stack([p[2] for p in pats])
    return jnp.where(ok[None], rpb[:, dr, dc].astype(F32), NEG_INF)


def _attn_kernel(qr_ref, qp_ref, k_ref, v_ref, kc_ref, vc_ref, bias_ref, o_ref):
    length = qr_ref.shape[1]
    rows = length // GRID_W
    n_tiles = rows // Q_ROWS
    n_q, n_k = Q_ROWS * GRID_W, K_ROWS * GRID_W
    lane = lax.broadcasted_iota(jnp.int32, (1, LANES), 1)
    kc = kc_ref[0]
    vc = vc_ref[0]

    def tile(t, carry):
        q0 = pl.multiple_of(t * n_q, n_q)
        k_row0 = jnp.clip(Q_ROWS * t - WIN_ROWS // 2, 0, rows - K_ROWS)
        ks = pl.multiple_of(k_row0 * GRID_W, Q_ROWS * GRID_W)
        pat = jnp.where(t == 0, 1, jnp.where(t == n_tiles - 1, 2, 0))
        qr = qr_ref[0, pl.ds(q0, n_q), :]
        qp = qp_ref[0, pl.ds(q0, n_q), :]
        kb = k_ref[0, pl.ds(ks, n_k), :]
        vb = v_ref[0, pl.ds(ks, n_k), :]
        outs = []
        for h in range(HEADS_PER_STEP):
            own = (lane // HEAD_DIM) == h
            zero = jnp.zeros_like(qr)
            s_lat = _dot_nt(jnp.where(own, qr, zero), kb) + bias_ref[h, pat]
            s_ctx = _dot_nt(jnp.where(own, qp, zero), kc)
            m = jnp.maximum(jnp.max(s_lat, axis=-1, keepdims=True),
                            jnp.max(s_ctx, axis=-1, keepdims=True))
            p_lat = jnp.exp(s_lat - m)
            p_ctx = jnp.exp(s_ctx - m)
            denom = (jnp.sum(p_lat, axis=-1, keepdims=True)
                     + jnp.sum(p_ctx, axis=-1, keepdims=True))
            o = _dot(p_lat.astype(BF16), vb) + _dot(p_ctx.astype(BF16), vc)
            outs.append(o * (1.0 / denom))
        o = outs[0]
        for h in range(1, HEADS_PER_STEP):
            o = jnp.where((lane // HEAD_DIM) == h, outs[h], o)
        o_ref[0, pl.ds(q0, n_q), :] = o.astype(BF16)
        return carry

    lax.fori_loop(0, n_tiles, tile, 0)


def _attention(qr, qp, kr, v, kc, vc, bias):
    batch, length, width = qr.shape
    lc = kc.shape[1]
    n_q, n_k = Q_ROWS * GRID_W, K_ROWS * GRID_W
    lat = pl.BlockSpec((1, length, LANES), lambda b, hp: (b, 0, hp))
    ctx = pl.BlockSpec((1, lc, LANES), lambda b, hp: (b, 0, hp))
    return pl.pallas_call(
        _attn_kernel,
        grid=(batch, width // LANES),
        in_specs=[lat, lat, lat, lat, ctx, ctx,
                  pl.BlockSpec((HEADS_PER_STEP, 3, n_q, n_k), lambda b, hp: (hp, 0, 0, 0))],
        out_specs=lat,
        out_shape=jax.ShapeDtypeStruct((batch, length, width), BF16),
        compiler_params=pltpu.CompilerParams(
            dimension_semantics=("arbitrary", "arbitrary"), vmem_limit_bytes=VMEM_LIMIT),
        name="attention",
    )(qr, qp, kr, v, kc, vc, bias)


def _merge_kernel(alpha, ya_ref, yb_ref, ga_ref, gb_ref, x_ref, lxg_ref, lxb_ref,
                  g1_ref, sh2_ref, sc2_ref, wpa_ref, wpb_ref, wo_ref, bo_ref,
                  l1g_ref, l1b_ref, rw_ref, rb_ref, x1_ref, u2_ref, lg_ref):
    a = _dot(ya_ref[0], wpa_ref[...])
    b = _dot(yb_ref[0], wpb_ref[...])
    m = ga_ref[0].astype(F32) * a + gb_ref[0].astype(F32) * b
    mix = _dot(m.astype(BF16), wo_ref[...]) + bo_ref[...]
    xn = _layer_norm(x_ref[0], lxg_ref[...], lxb_ref[...])
    x1 = _layer_norm(alpha * xn + g1_ref[0] * mix, l1g_ref[...], l1b_ref[...])
    x1_ref[0] = x1
    u2 = (x1 * (1.0 + sc2_ref[0]) + sh2_ref[0]).astype(BF16)
    u2_ref[0] = u2
    lg_ref[0] = _dot(u2, rw_ref[...]) + rb_ref[...]


def _merge(alpha, ya, yb, ga, gb, x, lxg, lxb, g1, sh2, sc2, wpa, wpb, wo, bo, l1g, l1b, rw, rb):
    batch, length, d = x.shape
    tm = min(MERGE_TM, length)
    row = lambda a: a.reshape(1, -1)
    tile = pl.BlockSpec((1, tm, d), lambda b, i: (b, i, 0))
    vec = pl.BlockSpec((1, d), lambda b, i: (0, 0))
    per_b = pl.BlockSpec((1, 1, d), lambda b, i: (b, 0, 0))
    mat = pl.BlockSpec((d, d), lambda b, i: (0, 0))
    return pl.pallas_call(
        functools.partial(_merge_kernel, alpha),
        grid=(batch, length // tm),
        in_specs=[tile, tile, tile, tile, tile, vec, vec, per_b, per_b, per_b,
                  mat, mat, mat, vec, vec, vec,
                  pl.BlockSpec((d, LANES), lambda b, i: (0, 0)),
                  pl.BlockSpec((1, LANES), lambda b, i: (0, 0))],
        out_specs=[tile, tile, pl.BlockSpec((1, tm, LANES), lambda b, i: (b, i, 0))],
        out_shape=[jax.ShapeDtypeStruct((batch, length, d), F32),
                   jax.ShapeDtypeStruct((batch, length, d), BF16),
                   jax.ShapeDtypeStruct((batch, length, LANES), F32)],
        compiler_params=pltpu.CompilerParams(
            dimension_semantics=("arbitrary", "arbitrary"), vmem_limit_bytes=VMEM_LIMIT),
        name="merge",
    )(ya, yb, ga, gb, x, row(lxg), row(lxb), g1, sh2, sc2, wpa, wpb, wo, row(bo),
      row(l1g), row(l1b), rw, rb)


def _expert_kernel(be_ref, nu_ref, xs_ref, w1_ref, b1_ref, w2_ref, b2_ref, o_ref, w1b, w2b):
    i = pl.program_id(0)
    e = be_ref[i]
    prev = be_ref[jnp.maximum(i - 1, 0)]

    @pl.when((i == 0) | (e != prev))
    def _():
        w1b[...] = w1_ref[0].astype(BF16)
        w2b[...] = w2_ref[0].astype(BF16)

    @pl.when(i < nu_ref[0])
    def _():
        z = _dot(xs_ref[...], w1b[...]) + b1_ref[0]
        zg = jnp.minimum(z[:, :EXPERT_FF], SWIGLU_LIMIT)
        zl = jnp.clip(z[:, EXPERT_FF:], -SWIGLU_LIMIT, SWIGLU_LIMIT)
        act = zg * _sigmoid(SWIGLU_ALPHA * zg) * (zl + 1.0)
        o_ref[...] = _dot(act.astype(BF16), w2b[...]) + b2_ref[0]

    @pl.when(i >= nu_ref[0])
    def _():
        o_ref[...] = jnp.zeros_like(o_ref)


def _experts(block_e, n_used, xs, w1, b1, w2, b2, tb):
    n_rows, d = xs.shape
    n_blocks = n_rows // tb
    ne, _, f2 = w1.shape
    ff = w2.shape[1]
    grid_spec = pltpu.PrefetchScalarGridSpec(
        num_scalar_prefetch=2,
        grid=(n_blocks,),
        in_specs=[
            pl.BlockSpec((tb, d), lambda i, be, nu: (i, 0)),
            pl.BlockSpec((1, d, f2), lambda i, be, nu: (be[i], 0, 0)),
            pl.BlockSpec((1, 1, f2), lambda i, be, nu: (be[i], 0, 0)),
            pl.BlockSpec((1, ff, d), lambda i, be, nu: (be[i], 0, 0)),
            pl.BlockSpec((1, 1, d), lambda i, be, nu: (be[i], 0, 0)),
        ],
        out_specs=pl.BlockSpec((tb, d), lambda i, be, nu: (i, 0)),
        scratch_shapes=[pltpu.VMEM((d, f2), BF16), pltpu.VMEM((ff, d), BF16)],
    )
    return pl.pallas_call(
        _expert_kernel,
        grid_spec=grid_spec,
        out_shape=jax.ShapeDtypeStruct((n_rows, d), F32),
        compiler_params=pltpu.CompilerParams(
            dimension_semantics=("arbitrary",), vmem_limit_bytes=VMEM_LIMIT),
        name="experts",
    )(block_e, n_used, xs, w1, b1.reshape(ne, 1, f2), w2, b2.reshape(ne, 1, d))


def _moe(u2, logits, w1, b1, w2, b2):
    n, d = u2.shape
    tb = min(EXPERT_TM, n)
    top_val, top_idx = lax.top_k(logits, TOP_K)
    gates = jax.nn.softmax(top_val, axis=-1)
    n_assign = n * TOP_K
    flat_e = top_idx.reshape(-1)
    order = jnp.argsort(flat_e)
    e_sorted = flat_e[order]
    counts = jnp.bincount(flat_e, length=N_EXPERTS)
    padded = (counts + tb - 1) // tb * tb
    start = jnp.cumsum(counts) - counts
    pend = jnp.cumsum(padded)
    pstart = pend - padded
    dest = pstart[e_sorted] + jnp.arange(n_assign) - start[e_sorted]
    n_blocks = (n_assign + N_EXPERTS * (tb - 1) + tb - 1) // tb
    n_rows = n_blocks * tb
    slot_tok = jnp.full((n_rows,), n, jnp.int32).at[dest].set((order // TOP_K).astype(jnp.int32))
    pos = jnp.zeros((n_assign,), jnp.int32).at[order].set(dest.astype(jnp.int32)).reshape(n, TOP_K)
    block_e = jnp.clip(jnp.searchsorted(pend, jnp.arange(n_blocks) * tb, side='right'),
                       0, N_EXPERTS - 1).astype(jnp.int32)
    n_used = (pend[-1] // tb).astype(jnp.int32).reshape(1)
    u2_pad = jnp.concatenate([u2, jnp.zeros((1, d), u2.dtype)], axis=0)
    xs = u2_pad[slot_tok]
    ys = _experts(block_e, n_used, xs, w1, b1, w2, b2, tb)
    return jnp.sum(ys[pos] * gates[:, :, None], axis=1)


def _final_kernel(alpha, x1_ref, ff_ref, g2_ref, lg_ref, lb_ref, o_ref):
    o_ref[0] = _layer_norm(alpha * x1_ref[0] + g2_ref[0] * ff_ref[0], lg_ref[...], lb_ref[...])


def _final(alpha, x1, ff, g2, lg, lb):
    batch, length, d = x1.shape
    tm = min(MERGE_TM, length)
    tile = pl.BlockSpec((1, tm, d), lambda b, i: (b, i, 0))
    vec = pl.BlockSpec((1, d), lambda b, i: (0, 0))
    return pl.pallas_call(
        functools.partial(_final_kernel, alpha),
        grid=(batch, length // tm),
        in_specs=[tile, tile, pl.BlockSpec((1, 1, d), lambda b, i: (b, 0, 0)), vec, vec],
        out_specs=tile,
        out_shape=jax.ShapeDtypeStruct((batch, length, d), F32),
        compiler_params=pltpu.CompilerParams(
            dimension_semantics=("arbitrary", "arbitrary"), vmem_limit_bytes=VMEM_LIMIT),
        name="final",
    )(x1, ff, g2, lg.reshape(1, d), lb.reshape(1, d))


def _layer(x, ctx, mods, w_in, b_in, glg, glb, ws, bs, rpb, w_pa, w_pb, w_out, b_out,
           l1g, l1b, l2g, l2b, rw, rb, w1, b1, w2, b2, ln_x, ln_ctx, alpha):
    batch, length, d = x.shape
    rows = length // GRID_W
    sh1, sc1, g1, sh2, sc2, g2 = [m[:batch].reshape(batch, 1, d) for m in mods]
    sh1c, sc1c = mods[0][batch], mods[1][batch]

    w_bf = w_in.astype(BF16)
    bs_full = jnp.broadcast_to(bs[:, :, None], (GMLP_GROUPS, CHUNK, GMLP_GROUP_CH)).astype(F32)
    cos, sin = _rope_tables(length)
    ya, qr, qp, kr, v, ga, gb = _in_proj(x, ln_x[0], ln_x[1], sh1, sc1, w_bf, b_in, glg, glb,
                                         ws.astype(BF16), bs_full, cos, sin)
    kc, vc = _ctx_kv(ctx, ln_ctx[0], ln_ctx[1], sh1c, sc1c, w_bf, b_in)
    yb = _attention(qr, qp, kr, v, kc, vc, _attn_bias_tables(rpb, rows))

    rw_pad = jnp.zeros((d, LANES), BF16).at[:, :N_EXPERTS].set(rw.astype(BF16))
    rb_pad = jnp.zeros((1, LANES), F32).at[0, :N_EXPERTS].set(rb)
    x1, u2, logits = _merge(alpha, ya, yb, ga, gb, x, ln_x[0], ln_x[1], g1, sh2, sc2,
                            w_pa.astype(BF16), w_pb.astype(BF16), w_out.astype(BF16), b_out,
                            l1g, l1b, rw_pad, rb_pad)
    ff = _moe(u2.reshape(batch * length, d), logits.reshape(batch * length, LANES)[:, :N_EXPERTS],
              w1, b1, w2, b2)
    return _final(alpha, x1, ff.reshape(batch, length, d), g2, l2g, l2b)


def kernel(x, c, ctx, c_ctx, ln_x_g, ln_x_b, ln_ctx_g, ln_ctx_b, w_ada, b_ada, w_in, b_in,
           gmlp_ln_g, gmlp_ln_b, gmlp_ws, gmlp_bs, na_rpb, w_pa, w_pb, w_out, b_out,
           ln1_g, ln1_b, ln2_g, ln2_b, router_w, router_b, exp_w1, exp_b1, exp_w2, exp_b2):
    depth = w_ada.shape[0]
    assert depth == 1, "the context stream update of non-final layers is not implemented"
    batch, length, d = x.shape
    alpha = (2.0 * depth) ** 0.25
    cond_rows = 16
    cond = jnp.zeros((cond_rows, d), F32).at[:batch].set(c).at[batch].set(c_ctx)
    l = 0
    m = _adaln(cond, w_ada[l], b_ada[l])
    mods = jnp.split(m, 6, axis=-1)
    return _layer(x, ctx, mods, w_in[l], b_in[l], gmlp_ln_g[l], gmlp_ln_b[l], gmlp_ws[l],
                  gmlp_bs[l], na_rpb[l], w_pa[l], w_pb[l], w_out[l], b_out[l],
                  ln1_g[l], ln1_b[l], ln2_g[l], ln2_b[l], router_w[l], router_b[l],
                  exp_w1[l], exp_b1[l], exp_w2[l], exp_b2[l],
                  (ln_x_g, ln_x_b), (ln_ctx_g, ln_ctx_b), alpha)
```
